```python
import math
import jax, jax.numpy as jnp
from jax import lax
import numpy as np

D_MODEL = 2048
BATCH = 2
SEQ = 8192
DEPTH = 1
DEC_BATCH = 128
DEC_SEQ = 8
PAST_LEN = 16384
PAGE_SIZE = 128

HEAD_DIM = 64
D_ATTN = D_MODEL // 2
N_HEADS = D_ATTN // HEAD_DIM
KV_HEADS = 4
GROUP = N_HEADS // KV_HEADS
KV_DIM = KV_HEADS * HEAD_DIM
WINDOW = 128
D_CONV = D_MODEL - D_ATTN
CONV_W = 3
D_FF = 4 * D_MODEL
N_BUCKETS = 32
MAX_EXACT = N_BUCKETS // 2
REL_MAX_DIST = 128
D_IN = D_ATTN + 2 * KV_DIM + 3 * D_CONV
ALPHA = (2.0 * DEPTH) ** 0.25
BETA = (8.0 * DEPTH) ** -0.25
LN_EPS = 1e-5
NEG = -1e30

kernel_name = "hymba_swa_sink_shortconv_deepnorm_adaln"


def _ln(x):
    xf = x.astype(jnp.float32)
    mu = jnp.mean(xf, axis=-1, keepdims=True)
    var = jnp.mean(jnp.square(xf - mu), axis=-1, keepdims=True)
    return (xf - mu) * lax.rsqrt(var + LN_EPS)


def _rel_bucket(dist):
    n = jnp.maximum(dist, 0)
    nf = jnp.maximum(n, 1).astype(jnp.float32)
    large = MAX_EXACT + (jnp.log(nf / MAX_EXACT) / math.log(REL_MAX_DIST / MAX_EXACT)
                         * (N_BUCKETS - MAX_EXACT)).astype(jnp.int32)
    large = jnp.minimum(large, N_BUCKETS - 1)
    return jnp.where(n < MAX_EXACT, n, large)


def _rel_bias(rel_table, dist):
    b = jnp.take(rel_table, _rel_bucket(dist), axis=0)
    b = jnp.transpose(b, (2, 0, 1)).astype(jnp.float32)
    return b.reshape(KV_HEADS, GROUP, dist.shape[0], dist.shape[1])


def _sink_attention(q, k, v, bias, mask, sinks):
    s = jnp.einsum('...qhgd,...khd->...hgqk', q, k).astype(jnp.float32) * (HEAD_DIM ** -0.5) + bias
    s = jnp.where(mask, s, NEG)
    sk = sinks.astype(jnp.float32).reshape(KV_HEADS, GROUP, 1, 1)
    m = jnp.maximum(jnp.max(s, axis=-1, keepdims=True), sk)
    p = jnp.exp(s - m)
    p = p / (jnp.sum(p, axis=-1, keepdims=True) + jnp.exp(sk - m))
    return jnp.einsum('...hgqk,...khd->...qhgd', p.astype(v.dtype), v)


def _attn_prompt(q, k, v, rel_table, sinks):
    B, T = q.shape[0], q.shape[1]
    nb = T // WINDOW
    qb = q.reshape(B, nb, WINDOW, KV_HEADS, GROUP, HEAD_DIM)
    kp = jnp.concatenate([jnp.zeros_like(k[:, :WINDOW]), k], axis=1).reshape(B, nb + 1, WINDOW, KV_HEADS, HEAD_DIM)
    vp = jnp.concatenate([jnp.zeros_like(v[:, :WINDOW]), v], axis=1).reshape(B, nb + 1, WINDOW, KV_HEADS, HEAD_DIM)
    kb = jnp.concatenate([kp[:, :-1], kp[:, 1:]], axis=2)
    vb = jnp.concatenate([vp[:, :-1], vp[:, 1:]], axis=2)
    qi = jnp.arange(WINDOW, dtype=jnp.int32)[:, None]
    ki = jnp.arange(2 * WINDOW, dtype=jnp.int32)[None, :]
    dist = qi + WINDOW - ki
    band = (dist >= 0) & (dist < WINDOW)
    kpos = jnp.arange(nb, dtype=jnp.int32)[:, None, None] * WINDOW - WINDOW + ki[None]
    mask = (band[None] & (kpos >= 0))[:, None, None]
    o = _sink_attention(qb, kb, vb, _rel_bias(rel_table, dist), mask, sinks)
    return o.reshape(B, T, D_ATTN), k[:, T - WINDOW:], v[:, T - WINDOW:]


def _attn_sample(q, k, v, cache_k, cache_v, rel_table, sinks):
    S = q.shape[1]
    Wb = cache_k.shape[1]
    kc = jnp.concatenate([cache_k.astype(k.dtype), k], axis=1)
    vc = jnp.concatenate([cache_v.astype(v.dtype), v], axis=1)
    dist = (Wb + jnp.arange(S, dtype=jnp.int32))[:, None] - jnp.arange(Wb + S, dtype=jnp.int32)[None, :]
    mask = (dist >= 0) & (dist < WINDOW)
    o = _sink_attention(q, kc, vc, _rel_bias(rel_table, dist), mask, sinks)
    return o.reshape(q.shape[0], S, D_ATTN), kc[:, S:], vc[:, S:]


def _short_conv(u, buf, conv_w):
    T = u.shape[1]
    up = jnp.concatenate([buf.astype(u.dtype), u], axis=1)
    out = conv_w[0] * up[:, 0:T]
    for i in range(1, CONV_W):
        out = out + conv_w[i] * up[:, i:i + T]
    return out, up[:, T:]


def _layer(x, c, attn_fn, conv_buf, w_ada, b_ada, w_in, conv_w, w_o, ln1_g, ln1_b, w_up, w_down, ln2_g, ln2_b):
    N, T = x.shape[0], x.shape[1]
    mod = (jax.nn.silu(c) @ w_ada + b_ada).astype(jnp.float32)[:, None, :]
    sh1, sc1, g1, sh2, sc2, g2 = jnp.split(mod, 6, axis=-1)
    h = (_ln(x) * (1.0 + sc1) + sh1).astype(x.dtype)
    proj = h @ w_in
    q, k, v, gb, gc, xc = jnp.split(
        proj, [D_ATTN, D_ATTN + KV_DIM, D_ATTN + 2 * KV_DIM,
               D_ATTN + 2 * KV_DIM + D_CONV, D_ATTN + 2 * KV_DIM + 2 * D_CONV], axis=-1)
    q = q.reshape(N, T, KV_HEADS, GROUP, HEAD_DIM)
    k = k.reshape(N, T, KV_HEADS, HEAD_DIM)
    v = v.reshape(N, T, KV_HEADS, HEAD_DIM)
    o_attn, k_new, v_new = attn_fn(q, k, v)
    conv_out, conv_new = _short_conv(gc * xc, conv_buf, conv_w)
    mix = jnp.concatenate([o_attn, gb * conv_out], axis=-1) @ w_o
    x1 = (_ln(ALPHA * x.astype(jnp.float32) + g1 * mix.astype(jnp.float32)) * ln1_g + ln1_b).astype(x.dtype)
    h2 = (_ln(x1) * (1.0 + sc2) + sh2).astype(x.dtype)
    ff = jnp.square(jax.nn.relu(h2 @ w_up)) @ w_down
    x2 = (_ln(ALPHA * x1.astype(jnp.float32) + g2 * ff.astype(jnp.float32)) * ln2_g + ln2_b).astype(x.dtype)
    return x2, k_new, v_new, conv_new


def setup_inputs(seed: int = 0) -> dict:
    key = jax.random.key(seed)
    ks = jax.random.split(key, 20)
    f32 = jnp.float32
    win_buf = min(WINDOW, PAST_LEN)
    col_scale = jnp.concatenate([
        jnp.ones((D_ATTN + KV_DIM,), f32), jnp.full((KV_DIM,), BETA, f32),
        jnp.ones((2 * D_CONV,), f32), jnp.full((D_CONV,), BETA, f32)])
    return {
        "x_prompt": jax.random.normal(ks[0], (BATCH, SEQ, D_MODEL), f32),
        "x_sample": jax.random.normal(ks[1], (DEC_BATCH, DEC_SEQ, D_MODEL), f32),
        "cache_k": jax.random.normal(ks[2], (DEC_BATCH, win_buf, KV_HEADS, HEAD_DIM), f32),
        "cache_v": jax.random.normal(ks[3], (DEC_BATCH, win_buf, KV_HEADS, HEAD_DIM), f32),
        "state_conv": jax.random.normal(ks[4], (DEC_BATCH, CONV_W - 1, D_CONV), f32),
        "c_prompt": jax.random.normal(ks[5], (BATCH, D_MODEL), f32),
        "c_sample": jax.random.normal(ks[6], (DEC_BATCH, D_MODEL), f32),
        "w_ada": jax.random.normal(ks[7], (D_MODEL, 6 * D_MODEL), f32) * (0.5 * D_MODEL ** -0.5),
        "b_ada": 0.01 * jax.random.normal(ks[8], (6 * D_MODEL,), f32),
        "w_in": jax.random.normal(ks[9], (D_MODEL, D_IN), f32) * (D_MODEL ** -0.5) * col_scale,
        "attn_sinks": 0.5 * jax.random.normal(ks[10], (N_HEADS,), f32),
        "rel_bias": 0.5 * jax.random.normal(ks[11], (N_BUCKETS, N_HEADS), f32),
        "conv_w": jax.random.normal(ks[12], (CONV_W, D_CONV), f32) * (CONV_W ** -0.5),
        "w_o": jax.random.normal(ks[13], (D_MODEL, D_MODEL), f32) * (D_MODEL ** -0.5) * BETA,
        "ln1_g": 1.0 + 0.02 * jax.random.normal(ks[14], (D_MODEL,), f32),
        "ln1_b": 0.02 * jax.random.normal(ks[15], (D_MODEL,), f32),
        "w_up": jax.random.normal(ks[16], (D_MODEL, D_FF), f32) * (D_MODEL ** -0.5) * BETA,
        "w_down": jax.random.normal(ks[17], (D_FF, D_MODEL), f32) * (D_FF ** -0.5) * BETA,
        "ln2_g": 1.0 + 0.02 * jax.random.normal(ks[18], (D_MODEL,), f32),
        "ln2_b": 0.02 * jax.random.normal(ks[19], (D_MODEL,), f32),
    }


def reference(x_prompt, x_sample, cache_k, cache_v, state_conv, c_prompt, c_sample,
              w_ada, b_ada, w_in, attn_sinks, rel_bias, conv_w, w_o, ln1_g, ln1_b,
              w_up, w_down, ln2_g, ln2_b):
    yp = x_prompt
    for _ in range(DEPTH):
        conv_buf_p = jnp.zeros((x_prompt.shape[0], CONV_W - 1, D_CONV), x_prompt.dtype)
        yp, k_prompt, v_prompt, conv_prompt = _layer(
            yp, c_prompt,
            lambda q, k, v: _attn_prompt(q, k, v, rel_bias, attn_sinks),
            conv_buf_p, w_ada, b_ada, w_in, conv_w, w_o, ln1_g, ln1_b, w_up, w_down, ln2_g, ln2_b)
    ys = x_sample
    for _ in range(DEPTH):
        ys, k_sample, v_sample, conv_sample = _layer(
            ys, c_sample,
            lambda q, k, v: _attn_sample(q, k, v, cache_k, cache_v, rel_bias, attn_sinks),
            state_conv, w_ada, b_ada, w_in, conv_w, w_o, ln1_g, ln1_b, w_up, w_down, ln2_g, ln2_b)
    return (yp, ys, k_prompt, v_prompt, conv_prompt, k_sample, v_sample, conv_sample)
```

```python
import functools
import math

import numpy as np
import jax
import jax.numpy as jnp
from jax import lax
from jax.experimental import pallas as pl
from jax.experimental.pallas import tpu as pltpu

F32 = jnp.float32
BF16 = jnp.bfloat16

D_MODEL = 2048
HEAD_DIM = 64
D_ATTN = 1024
N_HEADS = 16
KV_HEADS = 4
GROUP = 4
KV_DIM = 256
WINDOW = 128
D_CONV = 1024
CONV_W = 3
D_FF = 8192
N_BUCKETS = 32
MAX_EXACT = 16
REL_MAX_DIST = 128
D_IN = D_ATTN + 2 * KV_DIM + 3 * D_CONV
ALPHA = 2.0 ** 0.25
LN_EPS = 1e-5
NEG = -1e30
LANES = 128
HALF = LANES // 2

COL_Q, COL_GB, COL_GC, COL_XC, COL_KV = 0, 1024, 2048, 3072, 4096

VMEM_LIMIT = 58 * 1024 * 1024


def _bucket_thresholds():
    d = np.arange(MAX_EXACT, WINDOW, dtype=np.float32)
    large = MAX_EXACT + (np.log(d / np.float32(MAX_EXACT)) / np.float32(math.log(REL_MAX_DIST / MAX_EXACT))
                         * np.float32(N_BUCKETS - MAX_EXACT)).astype(np.int32)
    large = np.minimum(large, N_BUCKETS - 1)
    thr = []
    for b in range(MAX_EXACT + 1, N_BUCKETS):
        hit = np.nonzero(large >= b)[0]
        if hit.size:
            thr.append(int(hit[0]) + MAX_EXACT)
    return tuple(thr)


_BUCKET_THR = _bucket_thresholds()


def _bucket(dist):
    big = jnp.full(dist.shape, MAX_EXACT, jnp.int32)
    for t in _BUCKET_THR:
        big = big + (dist >= t).astype(jnp.int32)
    return jnp.where(dist < MAX_EXACT, dist, big)


def _table_lookup(bucket, tab_ref, head):
    out = jnp.zeros(bucket.shape, F32)
    for b in range(N_BUCKETS):
        out = jnp.where(bucket == b, tab_ref[b * N_HEADS + head], out)
    return out


def _ln(x):
    mu = jnp.mean(x, axis=-1, keepdims=True)
    xc = x - mu
    var = jnp.mean(xc * xc, axis=-1, keepdims=True)
    return xc * lax.rsqrt(var + LN_EPS)


def _ada_kernel(c_ref, w_ref, b_ref, o_ref):
    c = c_ref[...]
    a = (c * jax.nn.sigmoid(c)).astype(BF16)
    o_ref[...] = jnp.dot(a, w_ref[...].astype(BF16), preferred_element_type=F32) + b_ref[...]


def _ada(c_all, w_ada, b_ada, tn=1024):
    rows = c_all.shape[0]
    n = w_ada.shape[1]
    return pl.pallas_call(
        _ada_kernel,
        grid=(n // tn,),
        in_specs=[pl.BlockSpec((rows, D_MODEL), lambda j: (0, 0)),
                  pl.BlockSpec((D_MODEL, tn), lambda j: (0, j)),
                  pl.BlockSpec((1, tn), lambda j: (0, j))],
        out_specs=pl.BlockSpec((rows, tn), lambda j: (0, j)),
        out_shape=jax.ShapeDtypeStruct((rows, n), F32),
        compiler_params=pltpu.CompilerParams(dimension_semantics=("arbitrary",), vmem_limit_bytes=VMEM_LIMIT),
        name="ada",
    )(c_all, w_ada, b_ada.reshape(1, n))


def _inproj_kernel(x_ref, sh_ref, sc_ref, w_ref, o_ref, h_ref):
    @pl.when(pl.program_id(1) == 0)
    def _():
        x = x_ref[...]
        h = _ln(x) * (1.0 + sc_ref[...]) + sh_ref[...]
        h_ref[...] = h.reshape(h_ref.shape).astype(BF16)

    o_ref[...] = jnp.dot(h_ref[...], w_ref[...], preferred_element_type=F32)


def _inproj(x3, mod3, mod_row, w_in, g, r, tn=512):
    ng, rt, _ = x3.shape
    rows = ng * rt
    tm = g * r
    tiles_per_seq = rt // r
    n_m = rows // tm
    xmap = lambda i, j: (i // tiles_per_seq, i % tiles_per_seq, 0)
    return pl.pallas_call(
        _inproj_kernel,
        grid=(n_m, D_IN // tn),
        in_specs=[pl.BlockSpec((g, r, D_MODEL), xmap),
                  pl.BlockSpec((g, 1, D_MODEL), lambda i, j: (mod_row(i), 0, 0)),
                  pl.BlockSpec((g, 1, D_MODEL), lambda i, j: (mod_row(i), 0, 1)),
                  pl.BlockSpec((D_MODEL, tn), lambda i, j: (0, j))],
        out_specs=pl.BlockSpec((tm, tn), lambda i, j: (i, j)),
        out_shape=jax.ShapeDtypeStruct((rows, D_IN), F32),
        scratch_shapes=[pltpu.VMEM((tm, D_MODEL), BF16)],
        compiler_params=pltpu.CompilerParams(dimension_semantics=("arbitrary", "arbitrary"),
                                             vmem_limit_bytes=VMEM_LIMIT),
        name="inproj",
    )(x3, mod3, mod3, w_in)


def _half_lane_pair(col_f32, kv_in_high_half):
    lane = lax.broadcasted_iota(jnp.int32, col_f32.shape, 1)
    rolled = pltpu.roll(col_f32, HALF, axis=1)
    if kv_in_high_half:
        lo = jnp.where(lane < HALF, rolled, 0.0)
        hi = jnp.where(lane >= HALF, col_f32, 0.0)
    else:
        lo = jnp.where(lane < HALF, col_f32, 0.0)
        hi = jnp.where(lane >= HALF, rolled, 0.0)
    return lo.astype(BF16), hi.astype(BF16)


def _dot_t(a, b):
    return lax.dot_general(a, b, (((1,), (1,)), ((), ())), preferred_element_type=F32)


def _sink_softmax(s, sink):
    m = jnp.maximum(jnp.max(s, axis=-1, keepdims=True), sink)
    p = jnp.exp(s - m)
    den = jnp.sum(p, axis=-1, keepdims=True) + jnp.exp(sink - m)
    return p / den


def _mixp_kernel(tab_ref, sink_ref, cw_ref, q_ref, kvc_ref, kvp_ref, gb_ref, gc_ref, xc_ref, gch_ref, xch_ref,
                 o_ref, ul_ref, bias_ref):
    n = pl.program_id(1)
    W = WINDOW

    @pl.when((pl.program_id(0) == 0) & (n == 0))
    def _():
        qi = lax.broadcasted_iota(jnp.int32, (W, 2 * W), 0)
        kj = lax.broadcasted_iota(jnp.int32, (W, 2 * W), 1)
        dist = qi + W - kj
        band = (dist >= 0) & (dist < W)
        bucket = _bucket(dist)
        for h in range(N_HEADS):
            bias_ref[h] = jnp.where(band, _table_lookup(bucket, tab_ref, h), NEG)

    first = n == 0
    kcol = lax.broadcasted_iota(jnp.int32, (1, 2 * W), 1)
    prev_off = jnp.where((kcol < W) & first, NEG, 0.0)
    q = (q_ref[0] * (HEAD_DIM ** -0.5)).astype(BF16)
    kvc = kvc_ref[0]
    kvp = kvp_ref[0]
    kall = jnp.concatenate([kvp[:, :KV_DIM], kvc[:, :KV_DIM]], axis=0)
    vall = jnp.concatenate([kvp[:, KV_DIM:], kvc[:, KV_DIM:]], axis=0)
    for g in range(KV_HEADS):
        c, hi_half = g // 2, (g % 2) == 1
        k_lo, k_hi = _half_lane_pair(kall[:, c * LANES:(c + 1) * LANES], hi_half)
        v_lo, v_hi = _half_lane_pair(vall[:, c * LANES:(c + 1) * LANES], hi_half)
        qs = jnp.concatenate([q[:, (2 * g) * LANES:(2 * g + 1) * LANES],
                              q[:, (2 * g + 1) * LANES:(2 * g + 2) * LANES]], axis=0)
        ps = []
        for half, kk in ((0, k_lo), (1, k_hi)):
            s = _dot_t(qs, kk)
            pp = []
            for colsel in (0, 1):
                h = 4 * g + 2 * colsel + half
                sh = s[colsel * W:(colsel + 1) * W] + bias_ref[h] + prev_off
                pp.append(_sink_softmax(sh, sink_ref[h]))
            ps.append(jnp.concatenate(pp, axis=0).astype(BF16))
        o = (jnp.dot(ps[0], v_lo, preferred_element_type=F32)
             + jnp.dot(ps[1], v_hi, preferred_element_type=F32))
        o_ref[0, :, (2 * g) * LANES:(2 * g + 1) * LANES] = o[:W].astype(BF16)
        o_ref[0, :, (2 * g + 1) * LANES:(2 * g + 2) * LANES] = o[W:].astype(BF16)

    u = gc_ref[0] * xc_ref[0]
    hal = jnp.where(first, 0.0, gch_ref[0] * xch_ref[0])
    row = lax.broadcasted_iota(jnp.int32, u.shape, 0)
    um1 = jnp.where(row == 0, hal[7:8], pltpu.roll(u, 1, axis=0))
    um2 = jnp.where(row == 0, hal[6:7], jnp.where(row == 1, hal[7:8], pltpu.roll(u, 2, axis=0)))
    conv = cw_ref[0:1] * um2 + cw_ref[1:2] * um1 + cw_ref[2:3] * u
    o_ref[0, :, D_ATTN:] = (gb_ref[0] * conv).astype(BF16)
    ul_ref[0] = u[W - 8:]


def _mix_prompt(proj3, tab, sinks, conv_w):
    nb, t, _ = proj3.shape
    W = WINDOW
    nblk = t // W
    smem = pl.BlockSpec(memory_space=pltpu.SMEM)
    prev = lambda b, n: (b, jnp.maximum(n - 1, 0), COL_KV // 512)
    halo = lambda col: (lambda b, n: (b, jnp.maximum(n * (W // 8) - 1, 0), col // D_CONV))
    return pl.pallas_call(
        _mixp_kernel,
        grid=(nb, nblk),
        in_specs=[smem, smem,
                  pl.BlockSpec((CONV_W, D_CONV), lambda b, n: (0, 0)),
                  pl.BlockSpec((1, W, D_ATTN), lambda b, n: (b, n, 0)),
                  pl.BlockSpec((1, W, 2 * KV_DIM), lambda b, n: (b, n, COL_KV // 512)),
                  pl.BlockSpec((1, W, 2 * KV_DIM), prev),
                  pl.BlockSpec((1, W, D_CONV), lambda b, n: (b, n, COL_GB // D_CONV)),
                  pl.BlockSpec((1, W, D_CONV), lambda b, n: (b, n, COL_GC // D_CONV)),
                  pl.BlockSpec((1, W, D_CONV), lambda b, n: (b, n, COL_XC // D_CONV)),
                  pl.BlockSpec((1, 8, D_CONV), halo(COL_GC)),
                  pl.BlockSpec((1, 8, D_CONV), halo(COL_XC))],
        out_specs=[pl.BlockSpec((1, W, D_MODEL), lambda b, n: (b, n, 0)),
                   pl.BlockSpec((1, 8, D_CONV), lambda b, n: (b, 0, 0))],
        out_shape=[jax.ShapeDtypeStruct((nb, t, D_MODEL), BF16),
                   jax.ShapeDtypeStruct((nb, 8, D_CONV), F32)],
        scratch_shapes=[pltpu.VMEM((N_HEADS, W, 2 * W), F32)],
        compiler_params=pltpu.CompilerParams(dimension_semantics=("arbitrary", "arbitrary"),
                                             vmem_limit_bytes=VMEM_LIMIT),
        name="mix_prompt",
    )(tab, sinks, conv_w, proj3, proj3, proj3, proj3, proj3, proj3, proj3, proj3)


SEQ_CHUNK = 16


def _mixs_kernel(tab_ref, sink_ref, cw_ref, q_ref, kvn_ref, ck_ref, cv_ref, gb_ref, gc_ref, xc_ref, st_ref,
                 o_ref, ko_ref, vo_ref, so_ref, bc_ref, bn_ref, sk_ref):
    gs = SEQ_CHUNK
    s_len = q_ref.shape[1]
    wb = ck_ref.shape[1]
    rows = gs * 2 * s_len

    @pl.when(pl.program_id(0) == 0)
    def _():
        r = lax.broadcasted_iota(jnp.int32, (rows, LANES), 0)
        j = lax.broadcasted_iota(jnp.int32, (rows, LANES), 1)
        t = r & (s_len - 1)
        colsel = (r >> 3) & 1
        seq = r >> 4
        dist_c = wb + t - j
        ok_c = (dist_c >= 0) & (dist_c < WINDOW)
        bucket_c = _bucket(dist_c)
        dist_n = t - (j & (s_len - 1))
        ok_n = ((j >> 3) == seq) & (dist_n >= 0)
        bucket_n = _bucket(dist_n)
        for g in range(KV_HEADS):
            for half in (0, 1):
                gh = 2 * g + half
                h0, h1 = 4 * g + half, 4 * g + 2 + half
                bc = jnp.where(colsel == 0, _table_lookup(bucket_c, tab_ref, h0),
                               _table_lookup(bucket_c, tab_ref, h1))
                bn = jnp.where(colsel == 0, _table_lookup(bucket_n, tab_ref, h0),
                               _table_lookup(bucket_n, tab_ref, h1))
                bc_ref[gh] = jnp.where(ok_c, bc, NEG)
                bn_ref[gh] = jnp.where(ok_n, bn, NEG)
                sk_ref[gh] = jnp.where(colsel == 0, sink_ref[h0], sink_ref[h1])

    q = q_ref[...] * (HEAD_DIM ** -0.5)
    kvn = kvn_ref[...].reshape(gs * s_len, 2 * KV_DIM)
    for g in range(KV_HEADS):
        c, hi_half = g // 2, (g % 2) == 1
        sl = slice(c * LANES, (c + 1) * LANES)
        kn_lo, kn_hi = _half_lane_pair(kvn[:, sl], hi_half)
        vn_lo, vn_hi = _half_lane_pair(kvn[:, KV_DIM + c * LANES:KV_DIM + (c + 1) * LANES], hi_half)
        qs = jnp.concatenate([q[:, :, (2 * g) * LANES:(2 * g + 1) * LANES],
                              q[:, :, (2 * g + 1) * LANES:(2 * g + 2) * LANES]], axis=1)
        qs = qs.reshape(rows, LANES).astype(BF16)
        kc_ops, vc_ops = [], []
        for i in range(gs):
            kc_ops.append(_half_lane_pair(ck_ref[i, :, sl], hi_half))
            vc_ops.append(_half_lane_pair(cv_ref[i, :, sl], hi_half))
        o = None
        for half in (0, 1):
            gh = 2 * g + half
            s_c = jnp.concatenate(
                [_dot_t(qs[i * 2 * s_len:(i + 1) * 2 * s_len], kc_ops[i][half]) for i in range(gs)], axis=0)
            s_n = _dot_t(qs, (kn_lo, kn_hi)[half])
            s = jnp.concatenate([s_c + bc_ref[gh], s_n + bn_ref[gh]], axis=1)
            p = _sink_softmax(s, sk_ref[gh][:, :1]).astype(BF16)
            o_c = jnp.concatenate(
                [jnp.dot(p[i * 2 * s_len:(i + 1) * 2 * s_len, :wb], vc_ops[i][half], preferred_element_type=F32)
                 for i in range(gs)], axis=0)
            o_h = o_c + jnp.dot(p[:, wb:], (vn_lo, vn_hi)[half], preferred_element_type=F32)
            o = o_h if o is None else o + o_h
        o3 = o.reshape(gs, 2 * s_len, LANES)
        o_ref[:, (2 * g) * LANES:(2 * g + 1) * LANES] = o3[:, :s_len].reshape(gs * s_len, LANES).astype(BF16)
        o_ref[:, (2 * g + 1) * LANES:(2 * g + 2) * LANES] = o3[:, s_len:].reshape(gs * s_len, LANES).astype(BF16)

    ko_ref[:, :wb - s_len] = ck_ref[:, s_len:]
    ko_ref[:, wb - s_len:] = kvn_ref[:, :, :KV_DIM]
    vo_ref[:, :wb - s_len] = cv_ref[:, s_len:]
    vo_ref[:, wb - s_len:] = kvn_ref[:, :, KV_DIM:]

    u = gc_ref[...] * xc_ref[...]
    st = st_ref[...]
    t3 = lax.broadcasted_iota(jnp.int32, u.shape, 1)
    um1 = jnp.where(t3 == 0, st[:, 1:2], pltpu.roll(u, 1, axis=1))
    um2 = jnp.where(t3 == 0, st[:, 0:1], jnp.where(t3 == 1, st[:, 1:2], pltpu.roll(u, 2, axis=1)))
    cw = cw_ref[...]
    conv = cw[0:1][None] * um2 + cw[1:2][None] * um1 + cw[2:3][None] * u
    o_ref[:, D_ATTN:] = (gb_ref[...] * conv).reshape(gs * s_len, D_CONV).astype(BF16)
    so_ref[...] = u[:, s_len - 2:]


def _mix_sample(proj3, cache_k, cache_v, state_conv, tab, sinks, conv_w):
    ns, s_len, _ = proj3.shape
    wb = cache_k.shape[1]
    gs = SEQ_CHUNK
    smem = pl.BlockSpec(memory_space=pltpu.SMEM)
    pcol = lambda col, width: (lambda i: (i, 0, col // width))
    seq3 = lambda i: (i, 0, 0)
    rows = gs * 2 * s_len
    return pl.pallas_call(
        _mixs_kernel,
        grid=(ns // gs,),
        in_specs=[smem, smem,
                  pl.BlockSpec((CONV_W, D_CONV), lambda i: (0, 0)),
                  pl.BlockSpec((gs, s_len, D_ATTN), pcol(COL_Q, D_ATTN)),
                  pl.BlockSpec((gs, s_len, 2 * KV_DIM), pcol(COL_KV, 2 * KV_DIM)),
                  pl.BlockSpec((gs, wb, KV_DIM), seq3),
                  pl.BlockSpec((gs, wb, KV_DIM), seq3),
                  pl.BlockSpec((gs, s_len, D_CONV), pcol(COL_GB, D_CONV)),
                  pl.BlockSpec((gs, s_len, D_CONV), pcol(COL_GC, D_CONV)),
                  pl.BlockSpec((gs, s_len, D_CONV), pcol(COL_XC, D_CONV)),
                  pl.BlockSpec((gs, CONV_W - 1, D_CONV), seq3)],
        out_specs=[pl.BlockSpec((gs * s_len, D_MODEL), lambda i: (i, 0)),
                   pl.BlockSpec((gs, wb, KV_DIM), seq3),
                   pl.BlockSpec((gs, wb, KV_DIM), seq3),
                   pl.BlockSpec((gs, CONV_W - 1, D_CONV), seq3)],
        out_shape=[jax.ShapeDtypeStruct((ns * s_len, D_MODEL), BF16),
                   jax.ShapeDtypeStruct((ns, wb, KV_DIM), F32),
                   jax.ShapeDtypeStruct((ns, wb, KV_DIM), F32),
                   jax.ShapeDtypeStruct((ns, CONV_W - 1, D_CONV), F32)],
        scratch_shapes=[pltpu.VMEM((2 * KV_HEADS, rows, LANES), F32),
                        pltpu.VMEM((2 * KV_HEADS, rows, LANES), F32),
                        pltpu.VMEM((2 * KV_HEADS, rows, LANES), F32)],
        compiler_params=pltpu.CompilerParams(dimension_semantics=("arbitrary",), vmem_limit_bytes=VMEM_LIMIT),
        name="mix_sample",
    )(tab, sinks, conv_w, proj3, proj3, cache_k, cache_v, proj3, proj3, proj3, state_conv)


def _outproj_kernel(a_ref, x_ref, g1_ref, sh2_ref, sc2_ref, w_ref, lg_ref, lb_ref, x1_ref, h2_ref):
    mix = jnp.dot(a_ref[...], w_ref[...], preferred_element_type=F32)
    x = x_ref[...]
    y = ALPHA * x + g1_ref[...] * mix.reshape(x.shape)
    x1 = _ln(y) * lg_ref[...] + lb_ref[...]
    x1_ref[...] = x1
    h2 = _ln(x1) * (1.0 + sc2_ref[...]) + sh2_ref[...]
    h2_ref[...] = h2.reshape(h2_ref.shape).astype(BF16)


def _outproj(a2, x3, mod3, mod_row, w_o, ln_g, ln_b, g, r):
    ng, rt, _ = x3.shape
    rows = ng * rt
    tm = g * r
    tiles_per_seq = rt // r
    xmap = lambda i: (i // tiles_per_seq, i % tiles_per_seq, 0)
    mod = lambda col: pl.BlockSpec((g, 1, D_MODEL), lambda i: (mod_row(i), 0, col))
    vec = pl.BlockSpec((1, D_MODEL), lambda i: (0, 0))
    return pl.pallas_call(
        _outproj_kernel,
        grid=(rows // tm,),
        in_specs=[pl.BlockSpec((tm, D_MODEL), lambda i: (i, 0)),
                  pl.BlockSpec((g, r, D_MODEL), xmap),
                  mod(2), mod(3), mod(4),
                  pl.BlockSpec((D_MODEL, D_MODEL), lambda i: (0, 0)),
                  vec, vec],
        out_specs=[pl.BlockSpec((g, r, D_MODEL), xmap),
                   pl.BlockSpec((tm, D_MODEL), lambda i: (i, 0))],
        out_shape=[jax.ShapeDtypeStruct(x3.shape, F32),
                   jax.ShapeDtypeStruct((rows, D_MODEL), BF16)],
        compiler_params=pltpu.CompilerParams(dimension_semantics=("arbitrary",), vmem_limit_bytes=VMEM_LIMIT),
        name="outproj",
    )(a2, x3, mod3, mod3, mod3, w_o, ln_g.reshape(1, D_MODEL), ln_b.reshape(1, D_MODEL))


def _ffn_kernel(h_ref, x1_ref, g2_ref, wu_ref, wd_ref, lg_ref, lb_ref, o_ref, acc_ref):
    f = pl.program_id(1)
    a = jnp.dot(h_ref[...], wu_ref[...], preferred_element_type=F32)
    a = jnp.square(jnp.maximum(a, 0.0)).astype(BF16)
    part = jnp.dot(a, wd_ref[...], preferred_element_type=F32)

    @pl.when(f == 0)
    def _():
        acc_ref[...] = part

    @pl.when(f > 0)
    def _():
        acc_ref[...] += part

    @pl.when(f == pl.num_programs(1) - 1)
    def _():
        x1 = x1_ref[...]
        y = ALPHA * x1 + g2_ref[...] * acc_ref[...].reshape(x1.shape)
        o_ref[...] = _ln(y) * lg_ref[...] + lb_ref[...]


def _ffn(h2, x13, mod3, mod_row, w_up, w_down, ln_g, ln_b, g, r, tf=512):
    ng, rt, _ = x13.shape
    rows = ng * rt
    tm = g * r
    tiles_per_seq = rt // r
    xmap = lambda i, f: (i // tiles_per_seq, i % tiles_per_seq, 0)
    vec = pl.BlockSpec((1, D_MODEL), lambda i, f: (0, 0))
    return pl.pallas_call(
        _ffn_kernel,
        grid=(rows // tm, D_FF // tf),
        in_specs=[pl.BlockSpec((tm, D_MODEL), lambda i, f: (i, 0)),
                  pl.BlockSpec((g, r, D_MODEL), xmap),
                  pl.BlockSpec((g, 1, D_MODEL), lambda i, f: (mod_row(i), 0, 5)),
                  pl.BlockSpec((D_MODEL, tf), lambda i, f: (0, f)),
                  pl.BlockSpec((tf, D_MODEL), lambda i, f: (f, 0)),
                  vec, vec],
        out_specs=pl.BlockSpec((g, r, D_MODEL), xmap),
        out_shape=jax.ShapeDtypeStruct(x13.shape, F32),
        scratch_shapes=[pltpu.VMEM((tm, D_MODEL), F32)],
        compiler_params=pltpu.CompilerParams(dimension_semantics=("arbitrary", "arbitrary"),
                                             vmem_limit_bytes=VMEM_LIMIT),
        name="ffn",
    )(h2, x13, mod3, w_up, w_down, ln_g.reshape(1, D_MODEL), ln_b.reshape(1, D_MODEL))


def _layer(x3, proj_to_mix, mod3, mod_row, w_in, w_o, ln1_g, ln1_b, w_up, w_down, ln2_g, ln2_b, g, r, g2, r2):
    proj = _inproj(x3, mod3, mod_row(g, r, x3), w_in, g, r)
    a2, extras = proj_to_mix(proj)
    x1, h2 = _outproj(a2, x3, mod3, mod_row(g2, r2, x3), w_o, ln1_g, ln1_b, g2, r2)
    x2 = _ffn(h2, x1, mod3, mod_row(g, r, x3), w_up, w_down, ln2_g, ln2_b, g, r)
    return x2, proj, extras


def kernel(x_prompt, x_sample, cache_k, cache_v, state_conv, c_prompt, c_sample, w_ada, b_ada, w_in, attn_sinks,
           rel_bias, conv_w, w_o, ln1_g, ln1_b, w_up, w_down, ln2_g, ln2_b):
    nb, t, _ = x_prompt.shape
    ns, s_len, _ = x_sample.shape
    wb = cache_k.shape[1]
    assert t % 512 == 0 and ns % SEQ_CHUNK == 0 and s_len == 8 and wb == WINDOW and CONV_W - 1 <= s_len

    o_k, o_v, o_gb = D_ATTN, D_ATTN + KV_DIM, D_ATTN + 2 * KV_DIM
    perm = np.concatenate([np.arange(0, D_ATTN), np.arange(o_gb, D_IN), np.arange(o_k, o_gb)])
    w_in_b = w_in[:, perm].astype(BF16)
    w_o_b, w_up_b, w_down_b = w_o.astype(BF16), w_up.astype(BF16), w_down.astype(BF16)
    tab = rel_bias.reshape(N_BUCKETS * N_HEADS)

    pad = (-(ns + nb)) % 8
    c_all = jnp.concatenate([c_sample, c_prompt, jnp.zeros((pad, D_MODEL), F32)], axis=0)
    mod = _ada(c_all, w_ada, b_ada)
    mod3 = mod.reshape(mod.shape[0], 1, 6 * D_MODEL)

    def prompt_rows(g, r, x3):
        tiles = x3.shape[1] // r
        return lambda i: ns + i // tiles

    def prompt_mix(proj):
        a, ul = _mix_prompt(proj.reshape(nb, t, D_IN), tab, attn_sinks, conv_w)
        return a.reshape(nb * t, D_MODEL), ul

    yp, proj_p, ul = _layer(x_prompt, prompt_mix, mod3, prompt_rows, w_in_b, w_o_b, ln1_g, ln1_b, w_up_b, w_down_b,
                            ln2_g, ln2_b, 1, 512, 1, 256)
    proj_p = proj_p.reshape(nb, t, D_IN)
    k_prompt = proj_p[:, t - WINDOW:, COL_KV:COL_KV + KV_DIM].reshape(nb, WINDOW, KV_HEADS, HEAD_DIM)
    v_prompt = proj_p[:, t - WINDOW:, COL_KV + KV_DIM:].reshape(nb, WINDOW, KV_HEADS, HEAD_DIM)
    conv_prompt = ul[:, 8 - (CONV_W - 1):]

    def sample_rows(g, r, x3):
        return lambda i: i

    def sample_mix(proj):
        a, ko, vo, so = _mix_sample(proj.reshape(ns, s_len, D_IN), cache_k.reshape(ns, wb, KV_DIM),
                                    cache_v.reshape(ns, wb, KV_DIM), state_conv, tab, attn_sinks, conv_w)
        return a, (ko, vo, so)

    gq = 512 // s_len
    ys, _, (ko, vo, so) = _layer(x_sample, sample_mix, mod3, sample_rows, w_in_b, w_o_b, ln1_g, ln1_b, w_up_b,
                                 w_down_b, ln2_g, ln2_b, gq, s_len, gq // 2, s_len)
    k_sample = ko.reshape(ns, wb, KV_HEADS, HEAD_DIM)
    v_sample = vo.reshape(ns, wb, KV_HEADS, HEAD_DIM)
    return (yp, ys, k_prompt, v_prompt, conv_prompt, k_sample, v_sample, so)
```

```python
import functools
import math

import numpy as np
import jax
import jax.numpy as jnp
from jax import lax
from jax.experimental import pallas as pl
from jax.experimental.pallas import tpu as pltpu

F32 = jnp.float32
BF16 = jnp.bfloat16

D_MODEL = 2048
HEAD_DIM = 64
D_ATTN = 1024
N_HEADS = 16
KV_HEADS = 4
KV_DIM = 256
WINDOW = 128
D_CONV = 1024
CONV_W = 3
D_FF = 8192
N_BUCKETS = 32
MAX_EXACT = 16
REL_MAX_DIST = 128
D_IN = D_ATTN + 2 * KV_DIM + 3 * D_CONV
O_KV, O_GB, O_GC, O_XC = D_ATTN, D_ATTN + 2 * KV_DIM, D_ATTN + 2 * KV_DIM + D_CONV, D_ATTN + 2 * KV_DIM + 2 * D_CONV
ALPHA = 2.0 ** 0.25
LN_EPS = 1e-5
NEG = -1e30
Q_SCALE = HEAD_DIM ** -0.5
LANES = 128
SUBLANES = 8
HALF = LANES // 2
COL_CHUNK = 512
ROW_CHUNK = 256

VMEM_LIMIT = 58 * 1024 * 1024


def _bucket_thresholds():
    d = np.arange(MAX_EXACT, WINDOW, dtype=np.float32)
    large = MAX_EXACT + (np.log(d / np.float32(MAX_EXACT)) / np.float32(math.log(REL_MAX_DIST / MAX_EXACT))
                         * np.float32(N_BUCKETS - MAX_EXACT)).astype(np.int32)
    large = np.minimum(large, N_BUCKETS - 1)
    thr = []
    for b in range(MAX_EXACT + 1, N_BUCKETS):
        hit = np.nonzero(large >= b)[0]
        if hit.size:
            thr.append(int(hit[0]) + MAX_EXACT)
    return tuple(thr)


_BUCKET_THR = _bucket_thresholds()


def _bucket(dist):
    big = jnp.full(dist.shape, MAX_EXACT, jnp.int32)
    for t in _BUCKET_THR:
        big = big + (dist >= t).astype(jnp.int32)
    return jnp.where(dist < MAX_EXACT, dist, big)


def _table_lookup(bucket, tab_ref, head):
    out = jnp.zeros(bucket.shape, F32)
    for b in range(N_BUCKETS):
        out = jnp.where(bucket == b, tab_ref[b * N_HEADS + head], out)
    return out


def _ln(x):
    mu = jnp.mean(x, axis=-1, keepdims=True)
    xc = x - mu
    var = jnp.mean(xc * xc, axis=-1, keepdims=True)
    return xc * lax.rsqrt(var + LN_EPS)


def _resident(shape):
    return pl.BlockSpec(shape, lambda *_: (0,) * len(shape), pipeline_mode=pl.Buffered(1))


def _row_map(tiles_per_seq):
    return lambda i, *_: (i // tiles_per_seq, i % tiles_per_seq, 0)


def _params(n_axes):
    return pltpu.CompilerParams(dimension_semantics=("arbitrary",) * n_axes, vmem_limit_bytes=VMEM_LIMIT)


def _ada_kernel(c_ref, w_ref, b_ref, o_ref):
    c = c_ref[...]
    a = (c * jax.nn.sigmoid(c)).astype(BF16)
    o_ref[...] = jnp.dot(a, w_ref[...].astype(BF16), preferred_element_type=F32) + b_ref[...]


def _ada(c_all, w_ada, b_ada, tn=1024):
    rows = c_all.shape[0]
    n = w_ada.shape[1]
    return pl.pallas_call(
        _ada_kernel,
        grid=(n // tn,),
        in_specs=[pl.BlockSpec((rows, D_MODEL), lambda j: (0, 0)),
                  pl.BlockSpec((D_MODEL, tn), lambda j: (0, j)),
                  pl.BlockSpec((1, tn), lambda j: (0, j))],
        out_specs=pl.BlockSpec((rows, tn), lambda j: (0, j)),
        out_shape=jax.ShapeDtypeStruct((rows, n), F32),
        compiler_params=_params(1),
        name="ada",
    )(c_all, w_ada, b_ada.reshape(1, n))


def _roll_rows(u, k):
    if u.shape[0] == 1:
        return pltpu.roll(u[0], k, axis=0)[None]
    return pltpu.roll(u, k, axis=1)


def _inproj_body(x_ref, sh_ref, sc_ref, w_ref, cw_ref, q_ref, kv_ref, s_ref, hist_fn, tail_fn):
    x = x_ref[...]
    g, r, _ = x.shape
    h = (_ln(x) * (1.0 + sc_ref[...]) + sh_ref[...]).reshape(g * r, D_MODEL).astype(BF16)

    def proj(lo, n=COL_CHUNK):
        return jnp.dot(h, w_ref[:, lo:lo + n], preferred_element_type=F32)

    for lo in range(0, D_ATTN, COL_CHUNK):
        q_ref[:, lo:lo + COL_CHUNK] = (proj(lo) * Q_SCALE).astype(q_ref.dtype)
    kv_ref[...] = proj(O_KV, 2 * KV_DIM)
    t = lax.broadcasted_iota(jnp.int32, (g, r, COL_CHUNK), 1)
    for lo in range(0, D_CONV, COL_CHUNK):
        u = (proj(O_GC + lo) * proj(O_XC + lo)).reshape(g, r, COL_CHUNK)
        hist = hist_fn(lo)
        um1 = jnp.where(t == 0, hist[:, 1:2], _roll_rows(u, 1))
        um2 = jnp.where(t == 0, hist[:, 0:1], jnp.where(t == 1, hist[:, 1:2], _roll_rows(u, 2)))
        cw = cw_ref[:, lo:lo + COL_CHUNK]
        conv = cw[0:1][None] * um2 + cw[1:2][None] * um1 + cw[2:3][None] * u
        s_ref[:, lo:lo + COL_CHUNK] = (proj(O_GB + lo) * conv.reshape(g * r, COL_CHUNK)).astype(BF16)
        tail_fn(lo, u)


def _inproj_prompt_kernel(tiles_per_seq, x_ref, sh_ref, sc_ref, w_ref, cw_ref, q_ref, kv_ref, s_ref, ul_ref,
                          carry_ref):
    @pl.when(lax.rem(pl.program_id(0), tiles_per_seq) == 0)
    def _():
        carry_ref[...] = jnp.zeros(carry_ref.shape, F32)

    def hist_fn(lo):
        return carry_ref[SUBLANES - 2:, lo:lo + COL_CHUNK][None]

    def tail_fn(lo, u):
        tail = u[0, u.shape[1] - SUBLANES:]
        carry_ref[:, lo:lo + COL_CHUNK] = tail
        ul_ref[0, :, lo:lo + COL_CHUNK] = tail

    _inproj_body(x_ref, sh_ref, sc_ref, w_ref, cw_ref, q_ref, kv_ref, s_ref, hist_fn, tail_fn)


def _inproj_sample_kernel(x_ref, sh_ref, sc_ref, w_ref, cw_ref, st_ref, q_ref, kv_ref, s_ref, so_ref):
    def hist_fn(lo):
        return st_ref[:, :, lo:lo + COL_CHUNK]

    def tail_fn(lo, u):
        so_ref[:, :, lo:lo + COL_CHUNK] = u[:, u.shape[1] - (CONV_W - 1):]

    _inproj_body(x_ref, sh_ref, sc_ref, w_ref, cw_ref, q_ref, kv_ref, s_ref, hist_fn, tail_fn)


def _inproj(x3, mod3, mod_row, w_in, conv_w, g, r, state=None):
    ng, rt, _ = x3.shape
    rows, tm, tps = ng * rt, g * r, rt // r
    xmap = _row_map(tps)
    mod = lambda col: pl.BlockSpec((g, 1, D_MODEL), lambda i: (mod_row(i), 0, col))
    in_specs = [pl.BlockSpec((g, r, D_MODEL), xmap), mod(0), mod(1),
                _resident((D_MODEL, D_IN)), _resident((CONV_W, D_CONV))]
    row2 = lambda n: pl.BlockSpec((tm, n), lambda i: (i, 0))
    out_specs = [row2(D_ATTN), row2(2 * KV_DIM), row2(D_CONV)]
    args = [x3, mod3, mod3, w_in, conv_w]
    if state is None:
        kern = functools.partial(_inproj_prompt_kernel, tps)
        q_dtype = BF16
        out_specs.append(pl.BlockSpec((1, SUBLANES, D_CONV), lambda i: (i // tps, 0, 0)))
        tail_shape = jax.ShapeDtypeStruct((ng, SUBLANES, D_CONV), F32)
        scratch = [pltpu.VMEM((SUBLANES, D_CONV), F32)]
    else:
        kern = _inproj_sample_kernel
        q_dtype = F32
        hist_spec = pl.BlockSpec((g, CONV_W - 1, D_CONV), lambda i: (i, 0, 0))
        in_specs.append(hist_spec)
        args.append(state)
        out_specs.append(hist_spec)
        tail_shape = jax.ShapeDtypeStruct(state.shape, F32)
        scratch = []
    return pl.pallas_call(
        kern,
        grid=(rows // tm,),
        in_specs=in_specs,
        out_specs=out_specs,
        out_shape=[jax.ShapeDtypeStruct((rows, D_ATTN), q_dtype),
                   jax.ShapeDtypeStruct((rows, 2 * KV_DIM), F32),
                   jax.ShapeDtypeStruct((rows, D_CONV), BF16),
                   tail_shape],
        scratch_shapes=scratch,
        compiler_params=_params(1),
        name="inproj",
    )(*args)


def _half_lane_pair(col_f32, kv_in_high_half):
    lane = lax.broadcasted_iota(jnp.int32, col_f32.shape, 1)
    rolled = pltpu.roll(col_f32, HALF, axis=1)
    if kv_in_high_half:
        lo = jnp.where(lane < HALF, rolled, 0.0)
        hi = jnp.where(lane >= HALF, col_f32, 0.0)
    else:
        lo = jnp.where(lane < HALF, col_f32, 0.0)
        hi = jnp.where(lane >= HALF, rolled, 0.0)
    return lo.astype(BF16), hi.astype(BF16)


def _dot_t(a, b):
    return lax.dot_general(a, b, (((1,), (1,)), ((), ())), preferred_element_type=F32)


def _sink_softmax(s, sink):
    m = jnp.maximum(jnp.max(s, axis=-1, keepdims=True), sink)
    p = jnp.exp(s - m)
    den = jnp.sum(p, axis=-1, keepdims=True) + jnp.exp(sink - m)
    return p / den


def _mixp_kernel(tab_ref, sink_ref, q_ref, kvc_ref, kvp_ref, o_ref, bias_ref):
    n = pl.program_id(1)
    W = WINDOW

    @pl.when((pl.program_id(0) == 0) & (n == 0))
    def _():
        qi = lax.broadcasted_iota(jnp.int32, (W, 2 * W), 0)
        kj = lax.broadcasted_iota(jnp.int32, (W, 2 * W), 1)
        dist = qi + W - kj
        band = (dist >= 0) & (dist < W)
        bucket = _bucket(dist)
        for h in range(N_HEADS):
            bias_ref[h] = jnp.where(band, _table_lookup(bucket, tab_ref, h), NEG)

    kcol = lax.broadcasted_iota(jnp.int32, (1, 2 * W), 1)
    prev_off = jnp.where((kcol < W) & (n == 0), NEG, 0.0)
    q = q_ref[0]
    kvc = kvc_ref[0]
    kvp = kvp_ref[0]
    kall = jnp.concatenate([kvp[:, :KV_DIM], kvc[:, :KV_DIM]], axis=0)
    vall = jnp.concatenate([kvp[:, KV_DIM:], kvc[:, KV_DIM:]], axis=0)
    for g in range(KV_HEADS):
        c, hi_half = g // 2, (g % 2) == 1
        k_lo, k_hi = _half_lane_pair(kall[:, c * LANES:(c + 1) * LANES], hi_half)
        v_lo, v_hi = _half_lane_pair(vall[:, c * LANES:(c + 1) * LANES], hi_half)
        qs = jnp.concatenate([q[:, (2 * g) * LANES:(2 * g + 1) * LANES],
                              q[:, (2 * g + 1) * LANES:(2 * g + 2) * LANES]], axis=0)
        ps = []
        for half, kk in ((0, k_lo), (1, k_hi)):
            s = _dot_t(qs, kk)
            pp = []
            for colsel in (0, 1):
                h = 4 * g + 2 * colsel + half
                sh = s[colsel * W:(colsel + 1) * W] + bias_ref[h] + prev_off
                pp.append(_sink_softmax(sh, sink_ref[h]))
            ps.append(jnp.concatenate(pp, axis=0).astype(BF16))
        o = (jnp.dot(ps[0], v_lo, preferred_element_type=F32)
             + jnp.dot(ps[1], v_hi, preferred_element_type=F32))
        o_ref[0, :, (2 * g) * LANES:(2 * g + 1) * LANES] = o[:W].astype(BF16)
        o_ref[0, :, (2 * g + 1) * LANES:(2 * g + 2) * LANES] = o[W:].astype(BF16)


def _mix_prompt(q3, kv3, tab, sinks):
    nb, t, _ = q3.shape
    W = WINDOW
    smem = pl.BlockSpec(memory_space=pltpu.SMEM)
    return pl.pallas_call(
        _mixp_kernel,
        grid=(nb, t // W),
        in_specs=[smem, smem,
                  pl.BlockSpec((1, W, D_ATTN), lambda b, n: (b, n, 0)),
                  pl.BlockSpec((1, W, 2 * KV_DIM), lambda b, n: (b, n, 0)),
                  pl.BlockSpec((1, W, 2 * KV_DIM), lambda b, n: (b, jnp.maximum(n - 1, 0), 0))],
        out_specs=pl.BlockSpec((1, W, D_ATTN), lambda b, n: (b, n, 0)),
        out_shape=jax.ShapeDtypeStruct((nb, t, D_ATTN), BF16),
        scratch_shapes=[pltpu.VMEM((N_HEADS, W, 2 * W), F32)],
        compiler_params=_params(2),
        name="mix_prompt",
    )(tab, sinks, q3, kv3, kv3)


SEQ_CHUNK = 16


def _mixs_kernel(tab_ref, sink_ref, q_ref, kvn_ref, ck_ref, cv_ref, o_ref, ko_ref, vo_ref, bc_ref, bn_ref, sk_ref):
    gs = SEQ_CHUNK
    s_len = q_ref.shape[1]
    wb = ck_ref.shape[1]
    rows = gs * 2 * s_len

    @pl.when(pl.program_id(0) == 0)
    def _():
        r = lax.broadcasted_iota(jnp.int32, (rows, LANES), 0)
        j = lax.broadcasted_iota(jnp.int32, (rows, LANES), 1)
        t = r & (s_len - 1)
        colsel = (r >> 3) & 1
        seq = r >> 4
        dist_c = wb + t - j
        ok_c = (dist_c >= 0) & (dist_c < WINDOW)
        bucket_c = _bucket(dist_c)
        dist_n = t - (j & (s_len - 1))
        ok_n = ((j >> 3) == seq) & (dist_n >= 0)
        bucket_n = _bucket(dist_n)
        for g in range(KV_HEADS):
            for half in (0, 1):
                gh = 2 * g + half
                h0, h1 = 4 * g + half, 4 * g + 2 + half
                bc = jnp.where(colsel == 0, _table_lookup(bucket_c, tab_ref, h0),
                               _table_lookup(bucket_c, tab_ref, h1))
                bn = jnp.where(colsel == 0, _table_lookup(bucket_n, tab_ref, h0),
                               _table_lookup(bucket_n, tab_ref, h1))
                bc_ref[gh] = jnp.where(ok_c, bc, NEG)
                bn_ref[gh] = jnp.where(ok_n, bn, NEG)
                sk_ref[gh] = jnp.where(colsel == 0, sink_ref[h0], sink_ref[h1])

    q = q_ref[...]
    kvn = kvn_ref[...].reshape(gs * s_len, 2 * KV_DIM)
    for g in range(KV_HEADS):
        c, hi_half = g // 2, (g % 2) == 1
        sl = slice(c * LANES, (c + 1) * LANES)
        kn_lo, kn_hi = _half_lane_pair(kvn[:, sl], hi_half)
        vn_lo, vn_hi = _half_lane_pair(kvn[:, KV_DIM + c * LANES:KV_DIM + (c + 1) * LANES], hi_half)
        qs = jnp.concatenate([q[:, :, (2 * g) * LANES:(2 * g + 1) * LANES],
                              q[:, :, (2 * g + 1) * LANES:(2 * g + 2) * LANES]], axis=1)
        qs = qs.reshape(rows, LANES).astype(BF16)
        kc_ops, vc_ops = [], []
        for i in range(gs):
            kc_ops.append(_half_lane_pair(ck_ref[i, :, sl], hi_half))
            vc_ops.append(_half_lane_pair(cv_ref[i, :, sl], hi_half))
        o = None
        for half in (0, 1):
            gh = 2 * g + half
            s_c = jnp.concatenate(
                [_dot_t(qs[i * 2 * s_len:(i + 1) * 2 * s_len], kc_ops[i][half]) for i in range(gs)], axis=0)
            s_n = _dot_t(qs, (kn_lo, kn_hi)[half])
            s = jnp.concatenate([s_c + bc_ref[gh], s_n + bn_ref[gh]], axis=1)
            p = _sink_softmax(s, sk_ref[gh][:, :1]).astype(BF16)
            o_c = jnp.concatenate(
                [jnp.dot(p[i * 2 * s_len:(i + 1) * 2 * s_len, :wb], vc_ops[i][half], preferred_element_type=F32)
                 for i in range(gs)], axis=0)
            o_h = o_c + jnp.dot(p[:, wb:], (vn_lo, vn_hi)[half], preferred_element_type=F32)
            o = o_h if o is None else o + o_h
        o3 = o.reshape(gs, 2 * s_len, LANES)
        o_ref[:, (2 * g) * LANES:(2 * g + 1) * LANES] = o3[:, :s_len].reshape(gs * s_len, LANES).astype(BF16)
        o_ref[:, (2 * g + 1) * LANES:(2 * g + 2) * LANES] = o3[:, s_len:].reshape(gs * s_len, LANES).astype(BF16)

    ko_ref[:, :wb - s_len] = ck_ref[:, s_len:]
    ko_ref[:, wb - s_len:] = kvn_ref[:, :, :KV_DIM]
    vo_ref[:, :wb - s_len] = cv_ref[:, s_len:]
    vo_ref[:, wb - s_len:] = kvn_ref[:, :, KV_DIM:]


def _mix_sample(q3, kvn3, cache_k, cache_v, tab, sinks):
    ns, s_len, _ = q3.shape
    wb = cache_k.shape[1]
    gs = SEQ_CHUNK
    smem = pl.BlockSpec(memory_space=pltpu.SMEM)
    seq3 = lambda i: (i, 0, 0)
    rows = gs * 2 * s_len
    return pl.pallas_call(
        _mixs_kernel,
        grid=(ns // gs,),
        in_specs=[smem, smem,
                  pl.BlockSpec((gs, s_len, D_ATTN), seq3),
                  pl.BlockSpec((gs, s_len, 2 * KV_DIM), seq3),
                  pl.BlockSpec((gs, wb, KV_DIM), seq3),
                  pl.BlockSpec((gs, wb, KV_DIM), seq3)],
        out_specs=[pl.BlockSpec((gs * s_len, D_ATTN), lambda i: (i, 0)),
                   pl.BlockSpec((gs, wb, KV_DIM), seq3),
                   pl.BlockSpec((gs, wb, KV_DIM), seq3)],
        out_shape=[jax.ShapeDtypeStruct((ns * s_len, D_ATTN), BF16),
                   jax.ShapeDtypeStruct((ns, wb, KV_DIM), F32),
                   jax.ShapeDtypeStruct((ns, wb, KV_DIM), F32)],
        scratch_shapes=[pltpu.VMEM((2 * KV_HEADS, rows, LANES), F32),
                        pltpu.VMEM((2 * KV_HEADS, rows, LANES), F32),
                        pltpu.VMEM((2 * KV_HEADS, rows, LANES), F32)],
        compiler_params=_params(1),
        name="mix_sample",
    )(tab, sinks, q3, kvn3, cache_k, cache_v)


def _row_chunk(ref, c, rows):
    g, r, _ = ref.shape
    if g == 1:
        return ref.at[:, c * rows:(c + 1) * rows, :]
    gc = rows // r
    return ref.at[c * gc:(c + 1) * gc]


def _mod_chunk(ref, c, rows, r):
    if ref.shape[0] == 1:
        return ref[...]
    gc = rows // r
    return ref[c * gc:(c + 1) * gc]


def _outproj_kernel(a_ref, s_ref, x_ref, g1_ref, sh2_ref, sc2_ref, w_ref, lg_ref, lb_ref, x1_ref, h2_ref):
    g, r, _ = x_ref.shape
    rc = min(ROW_CHUNK, g * r)
    for c in range(g * r // rc):
        rows = slice(c * rc, (c + 1) * rc)
        mix = (jnp.dot(a_ref[rows, :], w_ref[:D_ATTN, :], preferred_element_type=F32)
               + jnp.dot(s_ref[rows, :], w_ref[D_ATTN:, :], preferred_element_type=F32))
        x = _row_chunk(x_ref, c, rc)[...]
        y = ALPHA * x + _mod_chunk(g1_ref, c, rc, r) * mix.reshape(x.shape)
        x1 = _ln(y) * lg_ref[...] + lb_ref[...]
        _row_chunk(x1_ref, c, rc)[...] = x1
        h2 = _ln(x1) * (1.0 + _mod_chunk(sc2_ref, c, rc, r)) + _mod_chunk(sh2_ref, c, rc, r)
        h2_ref[rows, :] = h2.reshape(rc, D_MODEL).astype(BF16)


def _outproj(a2, s2, x3, mod3, mod_row, w_o, ln_g, ln_b, g, r):
    ng, rt, _ = x3.shape
    rows, tm, tps = ng * rt, g * r, rt // r
    xmap = _row_map(tps)
    mod = lambda col: pl.BlockSpec((g, 1, D_MODEL), lambda i: (mod_row(i), 0, col))
    row2 = lambda n: pl.BlockSpec((tm, n), lambda i: (i, 0))
    return pl.pallas_call(
        _outproj_kernel,
        grid=(rows // tm,),
        in_specs=[row2(D_ATTN), row2(D_CONV),
                  pl.BlockSpec((g, r, D_MODEL), xmap),
                  mod(2), mod(3), mod(4),
                  _resident((D_MODEL, D_MODEL)), _resident((1, D_MODEL)), _resident((1, D_MODEL))],
        out_specs=[pl.BlockSpec((g, r, D_MODEL), xmap), row2(D_MODEL)],
        out_shape=[jax.ShapeDtypeStruct(x3.shape, F32),
                   jax.ShapeDtypeStruct((rows, D_MODEL), BF16)],
        compiler_params=_params(1),
        name="outproj",
    )(a2, s2, x3, mod3, mod3, mod3, w_o, ln_g.reshape(1, D_MODEL), ln_b.reshape(1, D_MODEL))


def _ffn_kernel(h_ref, x1_ref, g2_ref, wu_ref, wd_ref, lg_ref, lb_ref, o_ref):
    f = pl.program_id(1)

    @pl.when(f == 0)
    def _():
        o_ref[...] = jnp.zeros(o_ref.shape, F32)

    a = jnp.dot(h_ref[...], wu_ref[...], preferred_element_type=F32)
    a = jnp.square(jnp.maximum(a, 0.0)).astype(BF16)
    o_ref[...] += jnp.dot(a, wd_ref[...], preferred_element_type=F32).reshape(o_ref.shape)

    @pl.when(f == pl.num_programs(1) - 1)
    def _():
        y = ALPHA * x1_ref[...] + g2_ref[...] * o_ref[...]
        o_ref[...] = _ln(y) * lg_ref[...] + lb_ref[...]


def _ffn(h2, x13, mod3, mod_row, w_up, w_down, ln_g, ln_b, g, r, tf=512):
    ng, rt, _ = x13.shape
    rows, tm, tps = ng * rt, g * r, rt // r
    xmap = _row_map(tps)
    vec = pl.BlockSpec((1, D_MODEL), lambda i, f: (0, 0))
    return pl.pallas_call(
        _ffn_kernel,
        grid=(rows // tm, D_FF // tf),
        in_specs=[pl.BlockSpec((tm, D_MODEL), lambda i, f: (i, 0)),
                  pl.BlockSpec((g, r, D_MODEL), xmap),
                  pl.BlockSpec((g, 1, D_MODEL), lambda i, f: (mod_row(i), 0, 5)),
                  pl.BlockSpec((D_MODEL, tf), lambda i, f: (0, f)),
                  pl.BlockSpec((tf, D_MODEL), lambda i, f: (f, 0)),
                  vec, vec],
        out_specs=pl.BlockSpec((g, r, D_MODEL), xmap),
        out_shape=jax.ShapeDtypeStruct(x13.shape, F32),
        compiler_params=_params(2),
        name="ffn",
    )(h2, x13, mod3, w_up, w_down, ln_g.reshape(1, D_MODEL), ln_b.reshape(1, D_MODEL))


PROMPT_TILES = {"inproj": (1, 512), "outproj": (1, 512), "ffn": (1, 1024)}
SAMPLE_TILES = {"inproj": (64, 8), "outproj": (64, 8), "ffn": (128, 8)}


def kernel(x_prompt, x_sample, cache_k, cache_v, state_conv, c_prompt, c_sample, w_ada, b_ada, w_in, attn_sinks,
           rel_bias, conv_w, w_o, ln1_g, ln1_b, w_up, w_down, ln2_g, ln2_b):
    nb, t, _ = x_prompt.shape
    ns, s_len, _ = x_sample.shape
    wb = cache_k.shape[1]
    assert t % 1024 == 0 and ns % 128 == 0 and s_len == SUBLANES and wb == WINDOW

    w_in_b, w_o_b, w_up_b, w_down_b = (w.astype(BF16) for w in (w_in, w_o, w_up, w_down))
    tab = rel_bias.reshape(N_BUCKETS * N_HEADS)

    pad = (-(ns + nb)) % SUBLANES
    c_all = jnp.concatenate([c_sample, c_prompt, jnp.zeros((pad, D_MODEL), F32)], axis=0)
    mod = _ada(c_all, w_ada, b_ada)
    mod3 = mod.reshape(mod.shape[0], 1, 6 * D_MODEL)

    def rest_of_layer(x3, a2, s2, mod_row, tiles):
        (go, ro), (gf, rf) = tiles["outproj"], tiles["ffn"]
        x1, h2 = _outproj(a2, s2, x3, mod3, mod_row(ro), w_o_b, ln1_g, ln1_b, go, ro)
        return _ffn(h2, x1, mod3, mod_row(rf), w_up_b, w_down_b, ln2_g, ln2_b, gf, rf)

    prompt_row = lambda r: (lambda i: ns + i // (t // r))
    gi, ri = PROMPT_TILES["inproj"]
    q, kv, s2, ul = _inproj(x_prompt, mod3, prompt_row(ri), w_in_b, conv_w, gi, ri)
    kv3 = kv.reshape(nb, t, 2 * KV_DIM)
    a = _mix_prompt(q.reshape(nb, t, D_ATTN), kv3, tab, attn_sinks)
    yp = rest_of_layer(x_prompt, a.reshape(nb * t, D_ATTN), s2, prompt_row, PROMPT_TILES)
    k_prompt = kv3[:, t - WINDOW:, :KV_DIM].reshape(nb, WINDOW, KV_HEADS, HEAD_DIM)
    v_prompt = kv3[:, t - WINDOW:, KV_DIM:].reshape(nb, WINDOW, KV_HEADS, HEAD_DIM)
    conv_prompt = ul[:, SUBLANES - (CONV_W - 1):]

    sample_row = lambda r: (lambda i: i)
    gi, ri = SAMPLE_TILES["inproj"]
    q, kv, s2, conv_sample = _inproj(x_sample, mod3, sample_row(ri), w_in_b, conv_w, gi, ri, state=state_conv)
    a, ko, vo = _mix_sample(q.reshape(ns, s_len, D_ATTN), kv.reshape(ns, s_len, 2 * KV_DIM),
                            cache_k.reshape(ns, wb, KV_DIM), cache_v.reshape(ns, wb, KV_DIM), tab, attn_sinks)
    ys = rest_of_layer(x_sample, a, s2, sample_row, SAMPLE_TILES)
    k_sample = ko.reshape(ns, wb, KV_HEADS, HEAD_DIM)
    v_sample = vo.reshape(ns, wb, KV_HEADS, HEAD_DIM)
    return (yp, ys, k_prompt, v_prompt, conv_prompt, k_sample, v_sample, conv_sample)
```
